```python
import jax, jax.numpy as jnp
from jax import lax
import numpy as np

D_MODEL = 1024
BATCH = 4
SEQ = 8192
DEPTH = 2

N_AB_LAYERS = (DEPTH + 1) // 2
N_C_LAYERS = DEPTH // 2

DN_HEADS = 4
DN_HEAD_DIM = 128
DN_KEY = DN_HEADS * DN_HEAD_DIM
DN_VAL = DN_HEADS * DN_HEAD_DIM
DN_CONV = 4
DN_CHUNK = 64
SG_GROUPS = 4
SG_GROUP_DIM = 128
SG_WIDTH = SG_GROUPS * SG_GROUP_DIM
SG_CHUNK = 128
POOL_WINDOWS = (2, 4, 8, 16)
POOL_GROUP_DIM = D_MODEL // len(POOL_WINDOWS)
D_FF = 2816
NORM_EPS = 1e-6

MIX_WIDTH = DN_VAL + SG_WIDTH
PROJ_SIZES = (DN_KEY, DN_KEY, DN_VAL, DN_VAL, DN_HEADS, DN_HEADS, SG_WIDTH, SG_WIDTH)
IN_PROJ = int(sum(PROJ_SIZES))
SPLIT_POINTS = tuple(int(s) for s in np.cumsum(PROJ_SIZES)[:-1])
QKV_WIDTH = 2 * DN_KEY + DN_VAL

kernel_name = "hybrid_deltanet_sgmlp_pool_macaron"


def _rmsnorm(x, w):
    xf = x.astype(jnp.float32)
    y = xf * lax.rsqrt(jnp.mean(xf * xf, axis=-1, keepdims=True) + NORM_EPS)
    return (y * w.astype(jnp.float32)).astype(x.dtype)


def _l2norm(x):
    return x * lax.rsqrt(jnp.sum(x * x, axis=-1, keepdims=True) + NORM_EPS)


def _swiglu(h, w_in, w_out):
    gate, up = jnp.split(h @ w_in, 2, axis=-1)
    return (jax.nn.silu(gate) * up) @ w_out


def _causal_dwconv(x, w):
    c = x.shape[-1]
    return lax.conv_general_dilated(
        x, w[:, None, :].astype(x.dtype), window_strides=(1,), padding=[(w.shape[0] - 1, 0)],
        dimension_numbers=("NWC", "WIO", "NWC"), feature_group_count=c)


def _gated_delta_rule(q, k, v, g, beta):
    bsz, t, h, dk = q.shape
    dv = v.shape[-1]
    n, c = t // DN_CHUNK, DN_CHUNK

    def chunk(a):
        return jnp.moveaxis(a.reshape((bsz, n, c, h) + a.shape[3:]), 3, 1)

    q = chunk(_l2norm(q) * dk ** -0.5)
    k = chunk(_l2norm(k))
    v = chunk(v)
    beta = chunk(beta)
    g = jnp.cumsum(chunk(g), axis=-1)
    causal = jnp.tril(jnp.ones((c, c), dtype=bool))
    strict = jnp.tril(jnp.ones((c, c), dtype=bool), k=-1)
    diff = g[..., :, None] - g[..., None, :]
    decay = jnp.where(causal, jnp.exp(jnp.where(causal, diff, 0.0)), 0.0)
    k_beta = k * beta[..., None]
    lower = jnp.where(strict, jnp.einsum('bhncd,bhnsd->bhncs', k_beta, k) * decay, 0.0)
    eye = jnp.eye(c, dtype=q.dtype)
    rhs = jnp.concatenate([v * beta[..., None], k_beta * jnp.exp(g)[..., None]], axis=-1)
    sol = lax.linalg.triangular_solve(eye + lower, rhs, left_side=True, lower=True, unit_diagonal=True)
    u, w = sol[..., :dv], sol[..., dv:]
    attn = jnp.einsum('bhncd,bhnsd->bhncs', q, k) * decay
    q_dec = q * jnp.exp(g)[..., None]
    g_last = g[..., -1]
    k_tail = k * jnp.exp(g_last[..., None] - g)[..., None]

    def step(state, inp):
        q_i, w_i, u_i, a_i, k_i, gl_i = inp
        v_new = u_i - jnp.einsum('bhck,bhkv->bhcv', w_i, state)
        o_i = jnp.einsum('bhck,bhkv->bhcv', q_i, state) + jnp.einsum('bhcs,bhsv->bhcv', a_i, v_new)
        state = state * jnp.exp(gl_i)[..., None, None] + jnp.einsum('bhck,bhcv->bhkv', k_i, v_new)
        return state, o_i

    xs = tuple(jnp.moveaxis(a, 2, 0) for a in (q_dec, w, u, attn, k_tail, g_last))
    state0 = jnp.zeros((bsz, h, dk, dv), q.dtype)
    _, o = lax.scan(step, state0, xs)
    return jnp.transpose(o, (1, 0, 3, 2, 4)).reshape(bsz, t, h, dv)


def _hybrid_ab_mixer(h, w_in, conv_w, a_log, dt_bias, dn_norm, sg_norm, sg_w, sg_b, w_out):
    bsz, t, _ = h.shape
    q, k, v, z, b, a, su, sv = jnp.split(h @ w_in, SPLIT_POINTS, axis=-1)
    qkv = jax.nn.silu(_causal_dwconv(jnp.concatenate([q, k, v], axis=-1), conv_w)).astype(jnp.float32)
    q, k, v = jnp.split(qkv, (DN_KEY, 2 * DN_KEY), axis=-1)
    q = q.reshape(bsz, t, DN_HEADS, DN_HEAD_DIM)
    k = k.reshape(bsz, t, DN_HEADS, DN_HEAD_DIM)
    v = v.reshape(bsz, t, DN_HEADS, DN_HEAD_DIM)
    beta = jax.nn.sigmoid(b.astype(jnp.float32))
    g = -jnp.exp(a_log.astype(jnp.float32)) * jax.nn.softplus(a.astype(jnp.float32) + dt_bias.astype(jnp.float32))
    o = _gated_delta_rule(q, k, v, g, beta)
    zf = z.astype(jnp.float32).reshape(bsz, t, DN_HEADS, DN_HEAD_DIM)
    o = _rmsnorm(o, dn_norm) * jax.nn.silu(zf)
    o_a = o.reshape(bsz, t, DN_VAL).astype(h.dtype)
    su = jax.nn.gelu(su, approximate=False).reshape(bsz, t, SG_GROUPS, SG_GROUP_DIM)
    sv = _rmsnorm(jax.nn.gelu(sv, approximate=False).reshape(bsz, t, SG_GROUPS, SG_GROUP_DIM), sg_norm)
    sv = sv.reshape(bsz, t // SG_CHUNK, SG_CHUNK, SG_GROUPS, SG_GROUP_DIM)
    tri = jnp.tril(jnp.ones((SG_CHUNK, SG_CHUNK), dtype=bool))
    w_s = jnp.where(tri, sg_w, 0.0).astype(sv.dtype)
    mixed = jnp.einsum('gts,bnsgc->bntgc', w_s, sv) + jnp.transpose(sg_b)[:, :, None].astype(sv.dtype)
    o_b = (su * mixed.reshape(bsz, t, SG_GROUPS, SG_GROUP_DIM)).reshape(bsz, t, SG_WIDTH)
    return jnp.concatenate([o_a, o_b], axis=-1) @ w_out


def _pool_mixer(h, pool_w, pool_scale):
    t = h.shape[1]
    hf = h.astype(jnp.float32)
    csum = jnp.cumsum(hf, axis=1)
    pos = jnp.arange(1, t + 1)
    outs = []
    for gi, win in enumerate(POOL_WINDOWS):
        sl = slice(gi * POOL_GROUP_DIM, (gi + 1) * POOL_GROUP_DIM)
        cs = csum[..., sl]
        lag = jnp.pad(cs[:, :-win], ((0, 0), (win, 0), (0, 0)))
        count = jnp.minimum(pos, win).astype(jnp.float32)[None, :, None]
        pooled = ((cs - lag) / count - hf[..., sl]).astype(h.dtype)
        outs.append(pooled @ pool_w[gi])
    return jnp.concatenate(outs, axis=-1) * pool_scale


def setup_inputs(seed: int = 0) -> dict:
    key = jax.random.key(seed)
    ks = jax.random.split(key, 24)
    f32 = jnp.float32
    nrm = lambda k, s, sc: jax.random.normal(k, s, f32) * sc
    gain = lambda k, s: 1.0 + 0.02 * jax.random.normal(k, s, f32)
    x = jax.random.normal(ks[0], (BATCH, SEQ, D_MODEL), f32)
    ffn_norm1 = gain(ks[1], (DEPTH, D_MODEL))
    ffn1_w_in = nrm(ks[2], (DEPTH, D_MODEL, 2 * D_FF), D_MODEL ** -0.5)
    ffn1_w_out = nrm(ks[3], (DEPTH, D_FF, D_MODEL), D_FF ** -0.5)
    mix_norm = gain(ks[4], (DEPTH, D_MODEL))
    ffn_norm2 = gain(ks[5], (DEPTH, D_MODEL))
    ffn2_w_in = nrm(ks[6], (DEPTH, D_MODEL, 2 * D_FF), D_MODEL ** -0.5)
    ffn2_w_out = nrm(ks[7], (DEPTH, D_FF, D_MODEL), D_FF ** -0.5)
    ab_w_in = nrm(ks[8], (N_AB_LAYERS, D_MODEL, IN_PROJ), D_MODEL ** -0.5)
    dn_conv_w = nrm(ks[9], (N_AB_LAYERS, DN_CONV, QKV_WIDTH), DN_CONV ** -0.5)
    dn_a_log = jnp.log(jax.random.uniform(ks[10], (N_AB_LAYERS, DN_HEADS), f32, 1.0, 16.0))
    dt = jnp.exp(jax.random.uniform(ks[11], (N_AB_LAYERS, DN_HEADS), f32, np.log(1e-3), np.log(1e-1)))
    dn_dt_bias = dt + jnp.log(-jnp.expm1(-dt))
    dn_out_norm = gain(ks[12], (N_AB_LAYERS, DN_HEAD_DIM))
    sg_norm = gain(ks[13], (N_AB_LAYERS, SG_GROUPS, SG_GROUP_DIM))
    sg_w = nrm(ks[14], (N_AB_LAYERS, SG_GROUPS, SG_CHUNK, SG_CHUNK), SG_CHUNK ** -0.5)
    sg_b = 1.0 + 0.1 * jax.random.normal(ks[15], (N_AB_LAYERS, SG_GROUPS, SG_CHUNK), f32)
    ab_w_out = nrm(ks[16], (N_AB_LAYERS, MIX_WIDTH, D_MODEL), MIX_WIDTH ** -0.5)
    pool_w = nrm(ks[17], (N_C_LAYERS, len(POOL_WINDOWS), POOL_GROUP_DIM, POOL_GROUP_DIM), POOL_GROUP_DIM ** -0.5)
    pool_scale = gain(ks[18], (N_C_LAYERS, D_MODEL))
    final_norm = gain(ks[19], (D_MODEL,))
    return {"x": x, "ffn_norm1": ffn_norm1, "ffn1_w_in": ffn1_w_in, "ffn1_w_out": ffn1_w_out,
            "mix_norm": mix_norm, "ffn_norm2": ffn_norm2, "ffn2_w_in": ffn2_w_in, "ffn2_w_out": ffn2_w_out,
            "ab_w_in": ab_w_in, "dn_conv_w": dn_conv_w, "dn_a_log": dn_a_log, "dn_dt_bias": dn_dt_bias,
            "dn_out_norm": dn_out_norm, "sg_norm": sg_norm, "sg_w": sg_w, "sg_b": sg_b, "ab_w_out": ab_w_out,
            "pool_w": pool_w, "pool_scale": pool_scale, "final_norm": final_norm}


def reference(x, ffn_norm1, ffn1_w_in, ffn1_w_out, mix_norm, ffn_norm2, ffn2_w_in, ffn2_w_out,
              ab_w_in, dn_conv_w, dn_a_log, dn_dt_bias, dn_out_norm, sg_norm, sg_w, sg_b, ab_w_out,
              pool_w, pool_scale, final_norm):
    for l in range(DEPTH):
        x = x + 0.5 * _swiglu(_rmsnorm(x, ffn_norm1[l]), ffn1_w_in[l], ffn1_w_out[l])
        h = _rmsnorm(x, mix_norm[l])
        i = l // 2
        if l % 2 == 0:
            x = x + _hybrid_ab_mixer(h, ab_w_in[i], dn_conv_w[i], dn_a_log[i], dn_dt_bias[i], dn_out_norm[i],
                                     sg_norm[i], sg_w[i], sg_b[i], ab_w_out[i])
        else:
            x = x + _pool_mixer(h, pool_w[i], pool_scale[i])
        x = x + 0.5 * _swiglu(_rmsnorm(x, ffn_norm2[l]), ffn2_w_in[l], ffn2_w_out[l])
    return _rmsnorm(x, final_norm)
```

```python
import functools

import jax
import jax.numpy as jnp
from jax import lax
from jax.experimental import pallas as pl
from jax.experimental.pallas import tpu as pltpu

F32 = jnp.float32
BF16 = jnp.bfloat16

D_MODEL = 1024
D_FF = 2816
NORM_EPS = 1e-6

DN_HEADS = 4
DN_HEAD_DIM = 128
DN_WIDTH = DN_HEADS * DN_HEAD_DIM
DN_CONV = 4
DN_CHUNK = 64
QKV_WIDTH = 3 * DN_WIDTH
SG_GROUPS = 4
SG_GROUP_DIM = 128
SG_WIDTH = SG_GROUPS * SG_GROUP_DIM
SG_CHUNK = 128
POOL_WINDOWS = (2, 4, 8, 16)
POOL_GROUP_DIM = D_MODEL // len(POOL_WINDOWS)
POOL_HALO = 16

LANES = 128
SUBLANES = 8
COL_Z = QKV_WIDTH
COL_SU = COL_Z + DN_WIDTH
COL_SV = COL_SU + SG_WIDTH
COL_BA = COL_SV + SG_WIDTH
PROJ_COLS = COL_BA + LANES
LANE_BETA = 0
LANE_ALPHA = DN_HEADS

VMEM_LIMIT_BYTES = 56 * 1024 * 1024


def _rmsnorm(x, w):
    return x * lax.rsqrt(jnp.mean(x * x, axis=-1, keepdims=True) + NORM_EPS) * w


def _silu(x):
    return x * jax.nn.sigmoid(x)


def _gelu(x):
    return 0.5 * x * (1.0 + lax.erf(x * (2.0 ** -0.5)))


def _softplus(x):
    return jnp.maximum(x, 0.0) + jnp.log1p(jnp.exp(-jnp.abs(x)))


def _mm(a, b):
    return jnp.dot(a.astype(BF16), b.astype(BF16), preferred_element_type=F32)


def _mm_nt(a, b):
    return lax.dot_general(a.astype(BF16), b.astype(BF16), (((1,), (1,)), ((), ())),
                           preferred_element_type=F32)


def _mm_tn(a, b):
    return lax.dot_general(a.astype(BF16), b.astype(BF16), (((0,), (0,)), ((), ())),
                           preferred_element_type=F32)


def _mm_exact(a, b):
    return jnp.dot(a, b, preferred_element_type=F32, precision=lax.Precision.HIGHEST)


def _ffn_kernel(x_ref, nw_ref, win_ref, wout_ref, fw_ref, o_ref, *, apply_final_norm):
    x = x_ref[...]
    xn = _rmsnorm(x, nw_ref[...])
    h = _mm(xn, win_ref[...])
    act = _silu(h[:, :D_FF]) * h[:, D_FF:]
    out = x + 0.5 * _mm(act, wout_ref[...])
    if apply_final_norm:
        out = _rmsnorm(out, fw_ref[...])
    o_ref[...] = out


def _ffn(x2d, norm_w, w_in, w_out, final_w, *, apply_final_norm, tm=512):
    n = x2d.shape[0]
    const = lambda i: (0, 0)
    return pl.pallas_call(
        functools.partial(_ffn_kernel, apply_final_norm=apply_final_norm),
        out_shape=jax.ShapeDtypeStruct((n, D_MODEL), F32),
        grid=(n // tm,),
        in_specs=[
            pl.BlockSpec((tm, D_MODEL), lambda i: (i, 0)),
            pl.BlockSpec((1, D_MODEL), const),
            pl.BlockSpec((D_MODEL, 2 * D_FF), const, pipeline_mode=pl.Buffered(1)),
            pl.BlockSpec((D_FF, D_MODEL), const, pipeline_mode=pl.Buffered(1)),
            pl.BlockSpec((1, D_MODEL), const),
        ],
        out_specs=pl.BlockSpec((tm, D_MODEL), lambda i: (i, 0)),
        compiler_params=pltpu.CompilerParams(
            dimension_semantics=("arbitrary",), vmem_limit_bytes=VMEM_LIMIT_BYTES),
        name="ffn",
    )(x2d, norm_w.reshape(1, D_MODEL), w_in.astype(BF16), w_out.astype(BF16),
      final_w.reshape(1, D_MODEL))


def _ab_kernel(x_ref, nw_ref, win_ref, convw_ref, alog_ref, dtb_ref, dnw_ref, sgn_ref, sgw_ref,
               sgbt_ref, wout_ref, o_ref,
               conv_s, q_s, k_s, kb_s, rhs_s, qd_s, kt_s, dec_s, gc_s, oa_s, state_s, *, tt):
    c64 = DN_CHUNK
    n_chunks = tt // c64

    @pl.when(pl.program_id(1) == 0)
    def _():
        conv_s[0:SUBLANES, :] = jnp.zeros((SUBLANES, QKV_WIDTH), F32)
        state_s[...] = jnp.zeros(state_s.shape, F32)

    x = x_ref[0]
    proj = _mm(_rmsnorm(x, nw_ref[...]), win_ref[...])

    pre = proj[:, :QKV_WIDTH]
    conv_s[SUBLANES:SUBLANES + tt, :] = pre
    acc = convw_ref[DN_CONV - 1:DN_CONV, :] * pre
    for kk in range(DN_CONV - 1):
        off = SUBLANES - (DN_CONV - 1) + kk
        acc = acc + convw_ref[kk:kk + 1, :] * conv_s[off:off + tt, :]
    conv_s[0:SUBLANES, :] = conv_s[tt:tt + SUBLANES, :]
    qkv = _silu(acc)

    ba = proj[:, COL_BA:COL_BA + LANES]
    beta_all = jax.nn.sigmoid(ba)
    g_all = -jnp.exp(alog_ref[...]) * _softplus(ba + dtb_ref[...])

    row = lax.broadcasted_iota(jnp.int32, (tt, tt), 0)
    col = lax.broadcasted_iota(jnp.int32, (tt, tt), 1)
    same_chunk = (row // c64) == (col // c64)
    tril = jnp.where(same_chunk & (col <= row), 1.0, 0.0).astype(F32)
    gc = _mm_exact(tril, g_all)
    gc_s[...] = gc
    eg_all = jnp.exp(gc)
    gc3 = gc.reshape(n_chunks, c64, LANES)
    g_last = jnp.broadcast_to(gc3[:, c64 - 1:c64, :], gc3.shape).reshape(tt, LANES)
    ktf_all = jnp.exp(g_last - gc)

    width = DN_HEADS * c64
    lane = lax.broadcasted_iota(jnp.int32, (tt, width), 1)
    trow = lax.broadcasted_iota(jnp.int32, (tt, width), 0)
    g_head = jnp.broadcast_to(gc[:, LANE_ALPHA + DN_HEADS - 1:LANE_ALPHA + DN_HEADS], (tt, width))
    for hh in range(DN_HEADS - 2, -1, -1):
        g_head = jnp.where(lane < (hh + 1) * c64,
                           jnp.broadcast_to(gc[:, LANE_ALPHA + hh:LANE_ALPHA + hh + 1], (tt, width)), g_head)
    on_diag = (lane % c64) == (trow % c64)
    g_rows = _mm_exact(jnp.where(same_chunk, 1.0, 0.0).astype(F32), jnp.where(on_diag, g_head, 0.0))
    causal = (lane % c64) <= (trow % c64)
    dec_s[...] = jnp.where(causal, jnp.exp(jnp.where(causal, g_head - g_rows, 0.0)), 0.0)

    qk_scale = DN_HEAD_DIM ** -0.5
    for hh in range(DN_HEADS):
        sl = slice(hh * DN_HEAD_DIM, (hh + 1) * DN_HEAD_DIM)
        qh = qkv[:, sl]
        kh = qkv[:, DN_WIDTH + hh * DN_HEAD_DIM:DN_WIDTH + (hh + 1) * DN_HEAD_DIM]
        vh = qkv[:, 2 * DN_WIDTH + hh * DN_HEAD_DIM:2 * DN_WIDTH + (hh + 1) * DN_HEAD_DIM]
        qh = qh * lax.rsqrt(jnp.sum(qh * qh, axis=-1, keepdims=True) + NORM_EPS) * qk_scale
        kh = kh * lax.rsqrt(jnp.sum(kh * kh, axis=-1, keepdims=True) + NORM_EPS)
        beta = beta_all[:, LANE_BETA + hh:LANE_BETA + hh + 1]
        eg = eg_all[:, LANE_ALPHA + hh:LANE_ALPHA + hh + 1]
        kb = kh * beta
        q_s[:, sl] = qh.astype(BF16)
        k_s[:, sl] = kh.astype(BF16)
        kb_s[:, sl] = kb.astype(BF16)
        rhs_s[:, 2 * hh * DN_HEAD_DIM:(2 * hh + 1) * DN_HEAD_DIM] = (vh * beta).astype(BF16)
        rhs_s[:, (2 * hh + 1) * DN_HEAD_DIM:(2 * hh + 2) * DN_HEAD_DIM] = (kb * eg).astype(BF16)
        qd_s[:, sl] = (qh * eg).astype(BF16)
        kt_s[:, sl] = (kh * ktf_all[:, LANE_ALPHA + hh:LANE_ALPHA + hh + 1]).astype(BF16)

    ri = lax.broadcasted_iota(jnp.int32, (c64, c64), 0)
    ci = lax.broadcasted_iota(jnp.int32, (c64, c64), 1)
    strict = ci < ri
    eye = jnp.where(ci == ri, 1.0, 0.0).astype(F32)

    def chunk_body(c, carry):
        r0 = pl.multiple_of(c * c64, c64)
        rows = pl.ds(r0, c64)
        decay_last = jnp.exp(gc_s[pl.ds(r0 + c64 - 1, 1), :])
        for hh in range(DN_HEADS):
            sl = slice(hh * DN_HEAD_DIM, (hh + 1) * DN_HEAD_DIM)
            k_c = k_s[rows, sl]
            dec = dec_s[rows, hh * c64:(hh + 1) * c64]
            lower = jnp.where(strict, _mm_nt(kb_s[rows, sl], k_c) * dec, 0.0)
            attn = _mm_nt(q_s[rows, sl], k_c) * dec
            power = lower
            inv = eye - lower
            for _ in range(5):
                power = _mm(power, power)
                inv = inv + _mm(inv, power)
            sol = _mm(inv, rhs_s[rows, 2 * hh * DN_HEAD_DIM:(2 * hh + 2) * DN_HEAD_DIM])
            u = sol[:, :DN_HEAD_DIM]
            w = sol[:, DN_HEAD_DIM:]
            state = state_s[hh]
            ws = _mm(jnp.concatenate([w.astype(BF16), qd_s[rows, sl]], axis=0), state)
            v_new = u - ws[:c64]
            oa_s[rows, sl] = ws[c64:] + _mm(attn, v_new)
            state_s[hh] = (state * decay_last[:, LANE_ALPHA + hh:LANE_ALPHA + hh + 1]
                           + _mm_tn(kt_s[rows, sl], v_new))
        return carry

    lax.fori_loop(0, n_chunks, chunk_body, 0)

    oa = oa_s[...]
    outs = []
    for hh in range(DN_HEADS):
        sl = slice(hh * DN_HEAD_DIM, (hh + 1) * DN_HEAD_DIM)
        z = proj[:, COL_Z + hh * DN_HEAD_DIM:COL_Z + (hh + 1) * DN_HEAD_DIM]
        outs.append(_rmsnorm(oa[:, sl], dnw_ref[...]) * _silu(z))

    ri = lax.broadcasted_iota(jnp.int32, (SG_CHUNK, SG_CHUNK), 0)
    ci = lax.broadcasted_iota(jnp.int32, (SG_CHUNK, SG_CHUNK), 1)
    for gg in range(SG_GROUPS):
        su = _gelu(proj[:, COL_SU + gg * SG_GROUP_DIM:COL_SU + (gg + 1) * SG_GROUP_DIM])
        sv = _gelu(proj[:, COL_SV + gg * SG_GROUP_DIM:COL_SV + (gg + 1) * SG_GROUP_DIM])
        sv = _rmsnorm(sv, sgn_ref[gg:gg + 1, :])
        w_s = jnp.where(ci <= ri, sgw_ref[gg], 0.0)
        bias = sgbt_ref[:, gg:gg + 1]
        mixed = [_mm(w_s, sv[n * SG_CHUNK:(n + 1) * SG_CHUNK, :]) + bias for n in range(tt // SG_CHUNK)]
        outs.append(su * jnp.concatenate(mixed, axis=0))

    o_ref[0] = x + _mm(jnp.concatenate(outs, axis=-1), wout_ref[...])


def _ab_mixer(x, norm_w, w_in, conv_w, a_log, dt_bias, dn_norm, sg_norm, sg_w, sg_b, w_out, *, tt=512):
    bsz, t, _ = x.shape
    q, k, v, z, b, a, su, sv = jnp.split(w_in, (512, 1024, 1536, 2048, 2052, 2056, 2568), axis=-1)
    pad = jnp.zeros((D_MODEL, LANES - 2 * DN_HEADS), w_in.dtype)
    w_in_r = jnp.concatenate([q, k, v, z, su, sv, b, a, pad], axis=-1).astype(BF16)
    lane_pad = lambda p: jnp.zeros((1, LANES), F32).at[0, LANE_ALPHA:LANE_ALPHA + DN_HEADS].set(p)
    const2 = lambda bi, ti: (0, 0)
    const3 = lambda bi, ti: (0, 0, 0)
    once = dict(pipeline_mode=pl.Buffered(1))
    return pl.pallas_call(
        functools.partial(_ab_kernel, tt=tt),
        out_shape=jax.ShapeDtypeStruct(x.shape, F32),
        grid=(bsz, t // tt),
        in_specs=[
            pl.BlockSpec((1, tt, D_MODEL), lambda bi, ti: (bi, ti, 0)),
            pl.BlockSpec((1, D_MODEL), const2),
            pl.BlockSpec((D_MODEL, PROJ_COLS), const2, **once),
            pl.BlockSpec((DN_CONV, QKV_WIDTH), const2),
            pl.BlockSpec((1, LANES), const2),
            pl.BlockSpec((1, LANES), const2),
            pl.BlockSpec((1, DN_HEAD_DIM), const2),
            pl.BlockSpec((SG_GROUPS, SG_GROUP_DIM), const2),
            pl.BlockSpec((SG_GROUPS, SG_CHUNK, SG_CHUNK), const3),
            pl.BlockSpec((SG_CHUNK, SG_GROUPS), const2),
            pl.BlockSpec((D_MODEL, D_MODEL), const2, **once),
        ],
        out_specs=pl.BlockSpec((1, tt, D_MODEL), lambda bi, ti: (bi, ti, 0)),
        scratch_shapes=[
            pltpu.VMEM((tt + SUBLANES, QKV_WIDTH), F32),
            pltpu.VMEM((tt, DN_WIDTH), BF16),
            pltpu.VMEM((tt, DN_WIDTH), BF16),
            pltpu.VMEM((tt, DN_WIDTH), BF16),
            pltpu.VMEM((tt, 2 * DN_WIDTH), BF16),
            pltpu.VMEM((tt, DN_WIDTH), BF16),
            pltpu.VMEM((tt, DN_WIDTH), BF16),
            pltpu.VMEM((tt, DN_HEADS * DN_CHUNK), F32),
            pltpu.VMEM((tt, LANES), F32),
            pltpu.VMEM((tt, DN_WIDTH), F32),
            pltpu.VMEM((DN_HEADS, DN_HEAD_DIM, DN_HEAD_DIM), F32),
        ],
        compiler_params=pltpu.CompilerParams(
            dimension_semantics=("arbitrary", "arbitrary"), vmem_limit_bytes=VMEM_LIMIT_BYTES),
        name="ab_mix",
    )(x, norm_w.reshape(1, D_MODEL), w_in_r, conv_w, lane_pad(a_log), lane_pad(dt_bias),
      dn_norm.reshape(1, DN_HEAD_DIM), sg_norm, sg_w, jnp.transpose(sg_b), w_out.astype(BF16))


def _pool_kernel(x_ref, nw_ref, pw_ref, ps_ref, o_ref, h_s, *, tt):
    ti = pl.program_id(1)

    @pl.when(ti == 0)
    def _():
        h_s[0:POOL_HALO, :] = jnp.zeros((POOL_HALO, D_MODEL), F32)

    x = x_ref[0]
    h = _rmsnorm(x, nw_ref[...])
    h_s[POOL_HALO:POOL_HALO + tt, :] = h
    pos = ti * tt + lax.broadcasted_iota(jnp.int32, (tt, 1), 0) + 1
    outs = []
    for gi, win in enumerate(POOL_WINDOWS):
        sl = slice(gi * POOL_GROUP_DIM, (gi + 1) * POOL_GROUP_DIM)
        hg = h[:, sl]
        acc = hg
        for s in range(1, win):
            acc = acc + h_s[POOL_HALO - s:POOL_HALO - s + tt, sl]
        count = jnp.minimum(pos, win).astype(F32)
        outs.append(_mm(acc / count - hg, pw_ref[gi]))
    h_s[0:POOL_HALO, :] = h_s[tt:tt + POOL_HALO, :]
    o_ref[0] = x + jnp.concatenate(outs, axis=-1) * ps_ref[...]


def _pool_mixer(x, norm_w, pool_w, pool_scale, *, tt=512):
    bsz, t, _ = x.shape
    const2 = lambda bi, ti: (0, 0)
    return pl.pallas_call(
        functools.partial(_pool_kernel, tt=tt),
        out_shape=jax.ShapeDtypeStruct(x.shape, F32),
        grid=(bsz, t // tt),
        in_specs=[
            pl.BlockSpec((1, tt, D_MODEL), lambda bi, ti: (bi, ti, 0)),
            pl.BlockSpec((1, D_MODEL), const2),
            pl.BlockSpec((len(POOL_WINDOWS), POOL_GROUP_DIM, POOL_GROUP_DIM), lambda bi, ti: (0, 0, 0)),
            pl.BlockSpec((1, D_MODEL), const2),
        ],
        out_specs=pl.BlockSpec((1, tt, D_MODEL), lambda bi, ti: (bi, ti, 0)),
        scratch_shapes=[pltpu.VMEM((tt + POOL_HALO, D_MODEL), F32)],
        compiler_params=pltpu.CompilerParams(
            dimension_semantics=("arbitrary", "arbitrary"), vmem_limit_bytes=VMEM_LIMIT_BYTES),
        name="pool_mix",
    )(x, norm_w.reshape(1, D_MODEL), pool_w.astype(BF16), pool_scale.reshape(1, D_MODEL))


def kernel(x, ffn_norm1, ffn1_w_in, ffn1_w_out, mix_norm, ffn_norm2, ffn2_w_in, ffn2_w_out, ab_w_in, dn_conv_w, dn_a_log, dn_dt_bias, dn_out_norm, sg_norm, sg_w, sg_b, ab_w_out, pool_w, pool_scale, final_norm):
    bsz, t, d = x.shape
    depth = ffn_norm1.shape[0]
    flat = lambda a: a.reshape(bsz * t, d)
    for l in range(depth):
        x = _ffn(flat(x), ffn_norm1[l], ffn1_w_in[l], ffn1_w_out[l], final_norm,
                 apply_final_norm=False).reshape(bsz, t, d)
        i = l // 2
        if l % 2 == 0:
            x = _ab_mixer(x, mix_norm[l], ab_w_in[i], dn_conv_w[i], dn_a_log[i], dn_dt_bias[i],
                          dn_out_norm[i], sg_norm[i], sg_w[i], sg_b[i], ab_w_out[i])
        else:
            x = _pool_mixer(x, mix_norm[l], pool_w[i], pool_scale[i])
        x = _ffn(flat(x), ffn_norm2[l], ffn2_w_in[l], ffn2_w_out[l], final_norm,
                 apply_final_norm=(l == depth - 1)).reshape(bsz, t, d)
    return x
```

```python
import functools

import jax
import jax.numpy as jnp
from jax import lax
from jax.experimental import pallas as pl
from jax.experimental.pallas import tpu as pltpu

F32 = jnp.float32
BF16 = jnp.bfloat16

D_MODEL = 1024
D_FF = 2816
NORM_EPS = 1e-6

DN_HEADS = 4
DN_HEAD_DIM = 128
DN_WIDTH = DN_HEADS * DN_HEAD_DIM
DN_CONV = 4
DN_CHUNK = 128
DN_INV_BLOCK = DN_CHUNK // 2
DN_INV_DOUBLINGS = 5
QKV_WIDTH = 3 * DN_WIDTH
SG_GROUPS = 4
SG_GROUP_DIM = 128
SG_WIDTH = SG_GROUPS * SG_GROUP_DIM
SG_CHUNK = 128
POOL_WINDOWS = (2, 4, 8, 16)
POOL_GROUP_DIM = D_MODEL // len(POOL_WINDOWS)
POOL_HALO = 16

LANES = 128
SUBLANES = 8
COL_Z = QKV_WIDTH
COL_SU = COL_Z + DN_WIDTH
COL_SV = COL_SU + SG_WIDTH
COL_BA = COL_SV + SG_WIDTH
PROJ_COLS = COL_BA + LANES
LANE_BETA = 0
LANE_ALPHA = DN_HEADS

VMEM_LIMIT_BYTES = 56 * 1024 * 1024


def _rmsnorm(x, w):
    return x * lax.rsqrt(jnp.mean(x * x, axis=-1, keepdims=True) + NORM_EPS) * w


def _silu(x):
    return x * jax.nn.sigmoid(x)


def _gelu(x):
    return 0.5 * x * (1.0 + lax.erf(x * (2.0 ** -0.5)))


def _softplus(x):
    return jnp.maximum(x, 0.0) + jnp.log1p(jnp.exp(-jnp.abs(x)))


def _mm(a, b):
    return jnp.dot(a.astype(BF16), b.astype(BF16), preferred_element_type=F32)


def _mm_nt(a, b):
    return lax.dot_general(a.astype(BF16), b.astype(BF16), (((1,), (1,)), ((), ())),
                           preferred_element_type=F32)


def _mm_tn(a, b):
    return lax.dot_general(a.astype(BF16), b.astype(BF16), (((0,), (0,)), ((), ())),
                           preferred_element_type=F32)


def _split3(x):
    hi = x.astype(BF16)
    r1 = x - hi.astype(F32)
    mid = r1.astype(BF16)
    lo = (r1 - mid.astype(F32)).astype(BF16)
    return hi, mid, lo


def _ffn_kernel(x_ref, nw_ref, win_ref, wout_ref, fw_ref, o_ref, *, apply_final_norm):
    x = x_ref[...]
    xn = _rmsnorm(x, nw_ref[...])
    h = _mm(xn, win_ref[...])
    act = _silu(h[:, :D_FF]) * h[:, D_FF:]
    out = x + 0.5 * _mm(act, wout_ref[...])
    if apply_final_norm:
        out = _rmsnorm(out, fw_ref[...])
    o_ref[...] = out


def _ffn(x2d, norm_w, w_in, w_out, final_w, *, apply_final_norm, tm=512):
    n = x2d.shape[0]
    const = lambda i: (0, 0)
    return pl.pallas_call(
        functools.partial(_ffn_kernel, apply_final_norm=apply_final_norm),
        out_shape=jax.ShapeDtypeStruct((n, D_MODEL), F32),
        grid=(n // tm,),
        in_specs=[
            pl.BlockSpec((tm, D_MODEL), lambda i: (i, 0)),
            pl.BlockSpec((1, D_MODEL), const),
            pl.BlockSpec((D_MODEL, 2 * D_FF), const, pipeline_mode=pl.Buffered(1)),
            pl.BlockSpec((D_FF, D_MODEL), const, pipeline_mode=pl.Buffered(1)),
            pl.BlockSpec((1, D_MODEL), const),
        ],
        out_specs=pl.BlockSpec((tm, D_MODEL), lambda i: (i, 0)),
        compiler_params=pltpu.CompilerParams(
            dimension_semantics=("arbitrary",), vmem_limit_bytes=VMEM_LIMIT_BYTES),
        name="ffn",
    )(x2d, norm_w.reshape(1, D_MODEL), w_in.astype(BF16), w_out.astype(BF16),
      final_w.reshape(1, D_MODEL))


def _ab_kernel(x_ref, nw_ref, win_ref, convw_ref, alog_ref, dtb_ref, dnw_ref, sgn_ref, sgw_ref,
               sgbt_ref, wout_ref, o_ref, conv_s, state_s, *, tt):
    cs = DN_CHUNK
    n_chunks = tt // cs

    @pl.when(pl.program_id(1) == 0)
    def _():
        conv_s[0:SUBLANES, :] = jnp.zeros((SUBLANES, QKV_WIDTH), F32)
        state_s[...] = jnp.zeros(state_s.shape, F32)

    x = x_ref[0]
    proj = _mm(_rmsnorm(x, nw_ref[...]), win_ref[...])

    pre = proj[:, :QKV_WIDTH]
    conv_s[SUBLANES:SUBLANES + tt, :] = pre
    acc = convw_ref[DN_CONV - 1:DN_CONV, :] * pre
    for kk in range(DN_CONV - 1):
        off = SUBLANES - (DN_CONV - 1) + kk
        acc = acc + convw_ref[kk:kk + 1, :] * conv_s[off:off + tt, :]
    conv_s[0:SUBLANES, :] = conv_s[tt:tt + SUBLANES, :]
    qkv = _silu(acc)

    ba = proj[:, COL_BA:COL_BA + LANES]
    beta_all = jax.nn.sigmoid(ba)
    g_all = -jnp.exp(alog_ref[...]) * _softplus(ba + dtb_ref[...])

    ri = lax.broadcasted_iota(jnp.int32, (cs, cs), 0)
    ci = lax.broadcasted_iota(jnp.int32, (cs, cs), 1)
    causal = ci <= ri
    strict = ci < ri
    eye = jnp.where(ci == ri, 1.0, 0.0).astype(F32)

    tril = jnp.where(causal, 1.0, 0.0).astype(BF16)
    g3 = jnp.concatenate(_split3(g_all), axis=-1)
    gc_parts = []
    for c in range(n_chunks):
        s3 = jnp.dot(tril, g3[c * cs:(c + 1) * cs, :], preferred_element_type=F32)
        gc_parts.append(s3[:, :LANES] + s3[:, LANES:2 * LANES] + s3[:, 2 * LANES:])
    gc = jnp.concatenate(gc_parts, axis=0)
    gc_t = gc.T
    eg_all = jnp.exp(gc)
    gc3 = gc.reshape(n_chunks, cs, LANES)
    g_last = jnp.broadcast_to(gc3[:, cs - 1:cs, :], gc3.shape).reshape(tt, LANES)
    ktf_all = jnp.exp(g_last - gc)

    qk_scale = DN_HEAD_DIM ** -0.5
    heads = range(DN_HEADS)
    chunks = range(n_chunks)
    k_bf, kbq, rhs, qd, kt = [], [], [], [], []
    for hh in heads:
        la = LANE_ALPHA + hh
        qh = qkv[:, hh * DN_HEAD_DIM:(hh + 1) * DN_HEAD_DIM]
        kh = qkv[:, DN_WIDTH + hh * DN_HEAD_DIM:DN_WIDTH + (hh + 1) * DN_HEAD_DIM]
        vh = qkv[:, 2 * DN_WIDTH + hh * DN_HEAD_DIM:2 * DN_WIDTH + (hh + 1) * DN_HEAD_DIM]
        qh = qh * lax.rsqrt(jnp.sum(qh * qh, axis=-1, keepdims=True) + NORM_EPS) * qk_scale
        kh = kh * lax.rsqrt(jnp.sum(kh * kh, axis=-1, keepdims=True) + NORM_EPS)
        beta = beta_all[:, LANE_BETA + hh:LANE_BETA + hh + 1]
        eg = eg_all[:, la:la + 1]
        kb = kh * beta
        k_bf.append(kh.astype(BF16))
        kbq.append([jnp.concatenate([kb[c * cs:(c + 1) * cs].astype(BF16), qh[c * cs:(c + 1) * cs].astype(BF16)],
                                    axis=0) for c in chunks])
        rhs.append(jnp.concatenate([(vh * beta).astype(BF16), (kb * eg).astype(BF16)], axis=-1))
        qd.append((qh * eg).astype(BF16))
        kt.append((kh * ktf_all[:, la:la + 1]).astype(BF16))

    chains = [(hh, c) for c in chunks for hh in heads]
    rows_of = lambda c: slice(c * cs, (c + 1) * cs)
    power, inv, attn = {}, {}, {}
    for hh, c in chains:
        la = LANE_ALPHA + hh
        diff = gc[rows_of(c), la:la + 1] - gc_t[la:la + 1, rows_of(c)]
        dec = jnp.where(causal, jnp.exp(jnp.where(causal, diff, 0.0)), 0.0)
        kq = _mm_nt(kbq[hh][c], k_bf[hh][rows_of(c)])
        lower = jnp.where(strict, kq[:cs] * dec, 0.0)
        attn[hh, c] = kq[cs:] * dec
        power[hh, c] = lower
    same_half = (ri // DN_INV_BLOCK) == (ci // DN_INV_BLOCK)
    off_diag = {}
    for key in chains:
        off_diag[key] = jnp.where(same_half, 0.0, power[key])
        power[key] = jnp.where(same_half, power[key], 0.0)
        inv[key] = eye - power[key]
    for _ in range(DN_INV_DOUBLINGS):
        for key in chains:
            power[key] = _mm(power[key], power[key])
        for key in chains:
            inv[key] = inv[key] + _mm(inv[key], power[key])
    for key in chains:
        off_diag[key] = _mm(inv[key], off_diag[key])
    for key in chains:
        inv[key] = inv[key] - _mm(off_diag[key], inv[key])
    sol = {(hh, c): _mm(inv[hh, c], rhs[hh][rows_of(c)]) for hh, c in chains}

    state = [state_s[hh] for hh in heads]
    o_parts = [[] for _ in heads]
    for c in chunks:
        r_last = (c + 1) * cs - 1
        ws = [_mm(jnp.concatenate([sol[hh, c][:, DN_HEAD_DIM:].astype(BF16), qd[hh][rows_of(c)]], axis=0),
                  state[hh]) for hh in heads]
        v_new = [sol[hh, c][:, :DN_HEAD_DIM] - ws[hh][:cs] for hh in heads]
        for hh in heads:
            o_parts[hh].append(ws[hh][cs:] + _mm(attn[hh, c], v_new[hh]))
        for hh in heads:
            la = LANE_ALPHA + hh
            state[hh] = (state[hh] * jnp.exp(gc[r_last:r_last + 1, la:la + 1])
                         + _mm_tn(kt[hh][rows_of(c)], v_new[hh]))
    out_a = []
    for hh in heads:
        state_s[hh] = state[hh]
        z = proj[:, COL_Z + hh * DN_HEAD_DIM:COL_Z + (hh + 1) * DN_HEAD_DIM]
        out_a.append(_rmsnorm(jnp.concatenate(o_parts[hh], axis=0), dnw_ref[...]) * _silu(z))

    ri = lax.broadcasted_iota(jnp.int32, (SG_CHUNK, SG_CHUNK), 0)
    ci = lax.broadcasted_iota(jnp.int32, (SG_CHUNK, SG_CHUNK), 1)
    n_sg = tt // SG_CHUNK
    out_b = []
    for gg in range(SG_GROUPS):
        su = _gelu(proj[:, COL_SU + gg * SG_GROUP_DIM:COL_SU + (gg + 1) * SG_GROUP_DIM])
        sv = _gelu(proj[:, COL_SV + gg * SG_GROUP_DIM:COL_SV + (gg + 1) * SG_GROUP_DIM])
        sv = _rmsnorm(sv, sgn_ref[gg:gg + 1, :]).astype(BF16)
        w_s = jnp.where(ci <= ri, sgw_ref[gg], 0.0)
        sv_cat = jnp.concatenate([sv[n * SG_CHUNK:(n + 1) * SG_CHUNK, :] for n in range(n_sg)], axis=1)
        mixed_cat = _mm(w_s, sv_cat) + sgbt_ref[:, gg:gg + 1]
        mixed = jnp.concatenate([mixed_cat[:, n * SG_GROUP_DIM:(n + 1) * SG_GROUP_DIM] for n in range(n_sg)],
                                axis=0)
        out_b.append(su * mixed)

    o_ref[0] = x + _mm(jnp.concatenate(out_a + out_b, axis=-1), wout_ref[...])


def _ab_mixer(x, norm_w, w_in, conv_w, a_log, dt_bias, dn_norm, sg_norm, sg_w, sg_b, w_out, *, tt=512):
    bsz, t, _ = x.shape
    q, k, v, z, b, a, su, sv = jnp.split(w_in, (512, 1024, 1536, 2048, 2052, 2056, 2568), axis=-1)
    pad = jnp.zeros((D_MODEL, LANES - 2 * DN_HEADS), w_in.dtype)
    w_in_r = jnp.concatenate([q, k, v, z, su, sv, b, a, pad], axis=-1).astype(BF16)
    lane_pad = lambda p: jnp.zeros((1, LANES), F32).at[0, LANE_ALPHA:LANE_ALPHA + DN_HEADS].set(p)
    const2 = lambda bi, ti: (0, 0)
    const3 = lambda bi, ti: (0, 0, 0)
    once = dict(pipeline_mode=pl.Buffered(1))
    return pl.pallas_call(
        functools.partial(_ab_kernel, tt=tt),
        out_shape=jax.ShapeDtypeStruct(x.shape, F32),
        grid=(bsz, t // tt),
        in_specs=[
            pl.BlockSpec((1, tt, D_MODEL), lambda bi, ti: (bi, ti, 0)),
            pl.BlockSpec((1, D_MODEL), const2),
            pl.BlockSpec((D_MODEL, PROJ_COLS), const2, **once),
            pl.BlockSpec((DN_CONV, QKV_WIDTH), const2),
            pl.BlockSpec((1, LANES), const2),
            pl.BlockSpec((1, LANES), const2),
            pl.BlockSpec((1, DN_HEAD_DIM), const2),
            pl.BlockSpec((SG_GROUPS, SG_GROUP_DIM), const2),
            pl.BlockSpec((SG_GROUPS, SG_CHUNK, SG_CHUNK), const3),
            pl.BlockSpec((SG_CHUNK, SG_GROUPS), const2),
            pl.BlockSpec((D_MODEL, D_MODEL), const2, **once),
        ],
        out_specs=pl.BlockSpec((1, tt, D_MODEL), lambda bi, ti: (bi, ti, 0)),
        scratch_shapes=[
            pltpu.VMEM((tt + SUBLANES, QKV_WIDTH), F32),
            pltpu.VMEM((DN_HEADS, DN_HEAD_DIM, DN_HEAD_DIM), F32),
        ],
        compiler_params=pltpu.CompilerParams(
            dimension_semantics=("arbitrary", "arbitrary"), vmem_limit_bytes=VMEM_LIMIT_BYTES),
        name="ab_mix",
    )(x, norm_w.reshape(1, D_MODEL), w_in_r, conv_w, lane_pad(a_log), lane_pad(dt_bias),
      dn_norm.reshape(1, DN_HEAD_DIM), sg_norm, sg_w, jnp.transpose(sg_b), w_out.astype(BF16))


def _pool_kernel(x_ref, nw_ref, pw_ref, ps_ref, o_ref, h_s, *, tt):
    ti = pl.program_id(1)

    @pl.when(ti == 0)
    def _():
        h_s[0:POOL_HALO, :] = jnp.zeros((POOL_HALO, D_MODEL), F32)

    x = x_ref[0]
    h = _rmsnorm(x, nw_ref[...])
    h_s[POOL_HALO:POOL_HALO + tt, :] = h
    pos = ti * tt + lax.broadcasted_iota(jnp.int32, (tt, 1), 0) + 1
    outs = []
    for gi, win in enumerate(POOL_WINDOWS):
        sl = slice(gi * POOL_GROUP_DIM, (gi + 1) * POOL_GROUP_DIM)
        hg = h[:, sl]
        acc = hg
        for s in range(1, win):
            acc = acc + h_s[POOL_HALO - s:POOL_HALO - s + tt, sl]
        count = jnp.minimum(pos, win).astype(F32)
        outs.append(_mm(acc / count - hg, pw_ref[gi]))
    h_s[0:POOL_HALO, :] = h_s[tt:tt + POOL_HALO, :]
    o_ref[0] = x + jnp.concatenate(outs, axis=-1) * ps_ref[...]


def _pool_mixer(x, norm_w, pool_w, pool_scale, *, tt=512):
    bsz, t, _ = x.shape
    const2 = lambda bi, ti: (0, 0)
    return pl.pallas_call(
        functools.partial(_pool_kernel, tt=tt),
        out_shape=jax.ShapeDtypeStruct(x.shape, F32),
        grid=(bsz, t // tt),
        in_specs=[
            pl.BlockSpec((1, tt, D_MODEL), lambda bi, ti: (bi, ti, 0)),
            pl.BlockSpec((1, D_MODEL), const2),
            pl.BlockSpec((len(POOL_WINDOWS), POOL_GROUP_DIM, POOL_GROUP_DIM), lambda bi, ti: (0, 0, 0)),
            pl.BlockSpec((1, D_MODEL), const2),
        ],
        out_specs=pl.BlockSpec((1, tt, D_MODEL), lambda bi, ti: (bi, ti, 0)),
        scratch_shapes=[pltpu.VMEM((tt + POOL_HALO, D_MODEL), F32)],
        compiler_params=pltpu.CompilerParams(
            dimension_semantics=("arbitrary", "arbitrary"), vmem_limit_bytes=VMEM_LIMIT_BYTES),
        name="pool_mix",
    )(x, norm_w.reshape(1, D_MODEL), pool_w.astype(BF16), pool_scale.reshape(1, D_MODEL))


def kernel(x, ffn_norm1, ffn1_w_in, ffn1_w_out, mix_norm, ffn_norm2, ffn2_w_in, ffn2_w_out, ab_w_in, dn_conv_w, dn_a_log, dn_dt_bias, dn_out_norm, sg_norm, sg_w, sg_b, ab_w_out, pool_w, pool_scale, final_norm):
    bsz, t, d = x.shape
    depth = ffn_norm1.shape[0]
    flat = lambda a: a.reshape(bsz * t, d)
    for l in range(depth):
        x = _ffn(flat(x), ffn_norm1[l], ffn1_w_in[l], ffn1_w_out[l], final_norm,
                 apply_final_norm=False).reshape(bsz, t, d)
        i = l // 2
        if l % 2 == 0:
            x = _ab_mixer(x, mix_norm[l], ab_w_in[i], dn_conv_w[i], dn_a_log[i], dn_dt_bias[i],
                          dn_out_norm[i], sg_norm[i], sg_w[i], sg_b[i], ab_w_out[i])
        else:
            x = _pool_mixer(x, mix_norm[l], pool_w[i], pool_scale[i])
        x = _ffn(flat(x), ffn_norm2[l], ffn2_w_in[l], ffn2_w_out[l], final_norm,
                 apply_final_norm=(l == depth - 1)).reshape(bsz, t, d)
    return x
```

```python
import functools

import jax
import jax.numpy as jnp
from jax import lax
from jax.experimental import pallas as pl
from jax.experimental.pallas import tpu as pltpu

F32 = jnp.float32
BF16 = jnp.bfloat16

D_MODEL = 1024
D_FF = 2816
NORM_EPS = 1e-6

DN_HEADS = 4
DN_HEAD_DIM = 128
DN_WIDTH = DN_HEADS * DN_HEAD_DIM
DN_CONV = 4
DN_CHUNK = 128
DN_INV_BLOCK = DN_CHUNK // 2
DN_INV_DOUBLINGS = 5
QKV_WIDTH = 3 * DN_WIDTH
SG_GROUPS = 4
SG_GROUP_DIM = 128
SG_WIDTH = SG_GROUPS * SG_GROUP_DIM
SG_CHUNK = 128
POOL_WINDOWS = (2, 4, 8, 16)
POOL_GROUP_DIM = D_MODEL // len(POOL_WINDOWS)
POOL_HALO = 16

LANES = 128
SUBLANES = 8
COL_Z = QKV_WIDTH
COL_SU = COL_Z + DN_WIDTH
COL_SV = COL_SU + SG_WIDTH
COL_BA = COL_SV + SG_WIDTH
PROJ_COLS = COL_BA + LANES
LANE_BETA = 0
LANE_ALPHA = DN_HEADS

VMEM_LIMIT_BYTES = 56 * 1024 * 1024


def _rmsnorm(x, w):
    return x * lax.rsqrt(jnp.mean(x * x, axis=-1, keepdims=True) + NORM_EPS) * w


def _silu(x):
    return x * jax.nn.sigmoid(x)


def _gelu(x):
    return 0.5 * x * (1.0 + lax.erf(x * (2.0 ** -0.5)))


def _softplus(x):
    return jnp.maximum(x, 0.0) + jnp.log1p(jnp.exp(-jnp.abs(x)))


def _mm(a, b):
    return jnp.dot(a.astype(BF16), b.astype(BF16), preferred_element_type=F32)


def _mm_nt(a, b):
    return lax.dot_general(a.astype(BF16), b.astype(BF16), (((1,), (1,)), ((), ())),
                           preferred_element_type=F32)


def _mm_tn(a, b):
    return lax.dot_general(a.astype(BF16), b.astype(BF16), (((0,), (0,)), ((), ())),
                           preferred_element_type=F32)


def _split3(x):
    hi = x.astype(BF16)
    r1 = x - hi.astype(F32)
    mid = r1.astype(BF16)
    lo = (r1 - mid.astype(F32)).astype(BF16)
    return hi, mid, lo


def _ffn_kernel(x_ref, nw_ref, win_ref, wout_ref, fw_ref, o_ref, *, apply_final_norm):
    x = x_ref[...]
    xn = _rmsnorm(x, nw_ref[...])
    h = _mm(xn, win_ref[...])
    act = _silu(h[:, :D_FF]) * h[:, D_FF:]
    out = x + 0.5 * _mm(act, wout_ref[...])
    if apply_final_norm:
        out = _rmsnorm(out, fw_ref[...])
    o_ref[...] = out


def _ffn(x2d, norm_w, w_in, w_out, final_w, layer, *, apply_final_norm, tm=512):
    n = x2d.shape[0]
    layer_mat = lambda i: (layer, 0, 0)
    return pl.pallas_call(
        functools.partial(_ffn_kernel, apply_final_norm=apply_final_norm),
        out_shape=jax.ShapeDtypeStruct((n, D_MODEL), F32),
        grid=(n // tm,),
        in_specs=[
            pl.BlockSpec((tm, D_MODEL), lambda i: (i, 0)),
            pl.BlockSpec((None, 1, D_MODEL), layer_mat),
            pl.BlockSpec((None, D_MODEL, 2 * D_FF), layer_mat, pipeline_mode=pl.Buffered(1)),
            pl.BlockSpec((None, D_FF, D_MODEL), layer_mat, pipeline_mode=pl.Buffered(1)),
            pl.BlockSpec((1, D_MODEL), lambda i: (0, 0)),
        ],
        out_specs=pl.BlockSpec((tm, D_MODEL), lambda i: (i, 0)),
        compiler_params=pltpu.CompilerParams(
            dimension_semantics=("arbitrary",), vmem_limit_bytes=VMEM_LIMIT_BYTES),
        name="ffn",
    )(x2d, norm_w[:, None, :], w_in, w_out, final_w.reshape(1, D_MODEL))


def _ab_kernel(x_ref, nw_ref, win_ref, convw_ref, alog_ref, dtb_ref, dnw_ref, sgn_ref, sgw_ref,
               sgbt_ref, wout_ref, o_ref, conv_s, state_s, *, tt):
    cs = DN_CHUNK
    n_chunks = tt // cs

    @pl.when(pl.program_id(1) == 0)
    def _():
        conv_s[0:SUBLANES, :] = jnp.zeros((SUBLANES, QKV_WIDTH), F32)
        state_s[...] = jnp.zeros(state_s.shape, F32)

    x = x_ref[0]
    proj = _mm(_rmsnorm(x, nw_ref[...]), win_ref[...])

    pre = proj[:, :QKV_WIDTH]
    conv_s[SUBLANES:SUBLANES + tt, :] = pre
    acc = convw_ref[DN_CONV - 1:DN_CONV, :] * pre
    for kk in range(DN_CONV - 1):
        off = SUBLANES - (DN_CONV - 1) + kk
        acc = acc + convw_ref[kk:kk + 1, :] * conv_s[off:off + tt, :]
    conv_s[0:SUBLANES, :] = conv_s[tt:tt + SUBLANES, :]
    qkv = _silu(acc)

    ba = proj[:, COL_BA:COL_BA + LANES]
    beta_all = jax.nn.sigmoid(ba)
    g_all = -jnp.exp(alog_ref[...]) * _softplus(ba + dtb_ref[...])

    ri = lax.broadcasted_iota(jnp.int32, (cs, cs), 0)
    ci = lax.broadcasted_iota(jnp.int32, (cs, cs), 1)
    causal = ci <= ri
    strict = ci < ri
    eye = jnp.where(ci == ri, 1.0, 0.0).astype(F32)

    tril = jnp.where(causal, 1.0, 0.0).astype(BF16)
    g3 = jnp.concatenate(_split3(g_all), axis=-1)
    gc_parts = []
    for c in range(n_chunks):
        s3 = jnp.dot(tril, g3[c * cs:(c + 1) * cs, :], preferred_element_type=F32)
        gc_parts.append(s3[:, :LANES] + s3[:, LANES:2 * LANES] + s3[:, 2 * LANES:])
    gc = jnp.concatenate(gc_parts, axis=0)
    gc_t = gc.T
    eg_all = jnp.exp(gc)
    gc3 = gc.reshape(n_chunks, cs, LANES)
    g_last = jnp.broadcast_to(gc3[:, cs - 1:cs, :], gc3.shape).reshape(tt, LANES)
    ktf_all = jnp.exp(g_last - gc)

    qk_scale = DN_HEAD_DIM ** -0.5
    heads = range(DN_HEADS)
    chunks = range(n_chunks)
    k_bf, kbq, rhs, qd, kt = [], [], [], [], []
    for hh in heads:
        la = LANE_ALPHA + hh
        qh = qkv[:, hh * DN_HEAD_DIM:(hh + 1) * DN_HEAD_DIM]
        kh = qkv[:, DN_WIDTH + hh * DN_HEAD_DIM:DN_WIDTH + (hh + 1) * DN_HEAD_DIM]
        vh = qkv[:, 2 * DN_WIDTH + hh * DN_HEAD_DIM:2 * DN_WIDTH + (hh + 1) * DN_HEAD_DIM]
        qh = qh * lax.rsqrt(jnp.sum(qh * qh, axis=-1, keepdims=True) + NORM_EPS) * qk_scale
        kh = kh * lax.rsqrt(jnp.sum(kh * kh, axis=-1, keepdims=True) + NORM_EPS)
        beta = beta_all[:, LANE_BETA + hh:LANE_BETA + hh + 1]
        eg = eg_all[:, la:la + 1]
        kb = kh * beta
        k_bf.append(kh.astype(BF16))
        kbq.append([jnp.concatenate([kb[c * cs:(c + 1) * cs].astype(BF16), qh[c * cs:(c + 1) * cs].astype(BF16)],
                                    axis=0) for c in chunks])
        rhs.append(jnp.concatenate([(vh * beta).astype(BF16), (kb * eg).astype(BF16)], axis=-1))
        qd.append((qh * eg).astype(BF16))
        kt.append((kh * ktf_all[:, la:la + 1]).astype(BF16))

    pairs = [(2 * p, 2 * p + 1) for p in range(DN_HEADS // 2)]
    chains = [(p, c) for c in chunks for p in range(len(pairs))]
    rows_of = lambda c: slice(c * cs, (c + 1) * cs)
    side = lambda parts: jnp.concatenate(parts, axis=1)
    halves = lambda x2: (x2[:, :DN_HEAD_DIM], x2[:, DN_HEAD_DIM:])

    def bdiag(x2):
        xa, xb = halves(x2.astype(BF16))
        zero = jnp.zeros_like(xa)
        return jnp.concatenate([side([xa, zero]), side([zero, xb])], axis=0)

    causal2, strict2, eye2 = side([causal] * 2), side([strict] * 2), side([eye] * 2)
    same_half2 = side([(ri // DN_INV_BLOCK) == (ci // DN_INV_BLOCK)] * 2)
    power, inv, attn, off_diag = {}, {}, {}, {}
    for p, c in chains:
        diff = side([gc[rows_of(c), LANE_ALPHA + hh:LANE_ALPHA + hh + 1]
                     - gc_t[LANE_ALPHA + hh:LANE_ALPHA + hh + 1, rows_of(c)] for hh in pairs[p]])
        dec = jnp.where(causal2, jnp.exp(jnp.where(causal2, diff, 0.0)), 0.0)
        k2 = side([k_bf[hh][rows_of(c)] for hh in pairs[p]])
        kq = _mm_nt(side([kbq[hh][c] for hh in pairs[p]]), bdiag(k2))
        lower = jnp.where(strict2, kq[:cs] * dec, 0.0)
        attn[p, c] = kq[cs:] * dec
        off_diag[p, c] = jnp.where(same_half2, 0.0, lower)
        power[p, c] = jnp.where(same_half2, lower, 0.0)
        inv[p, c] = eye2 - power[p, c]
    for _ in range(DN_INV_DOUBLINGS):
        for key in chains:
            power[key] = _mm(power[key], bdiag(power[key]))
        for key in chains:
            inv[key] = inv[key] + _mm(inv[key], bdiag(power[key]))
    for key in chains:
        off_diag[key] = _mm(inv[key], bdiag(off_diag[key]))
    for key in chains:
        inv[key] = inv[key] - _mm(off_diag[key], bdiag(inv[key]))
    u2, w2 = {}, {}
    for p, c in chains:
        sol = [_mm(inv_h, rhs[hh][rows_of(c)]) for inv_h, hh in zip(halves(inv[p, c]), pairs[p])]
        u2[p, c] = side([s_h[:, :DN_HEAD_DIM] for s_h in sol])
        w2[p, c] = side([s_h[:, DN_HEAD_DIM:] for s_h in sol])

    width2 = 2 * DN_HEAD_DIM
    r2 = lax.broadcasted_iota(jnp.int32, (width2, width2), 0)
    c2 = lax.broadcasted_iota(jnp.int32, (width2, width2), 1)
    on_blocks = (r2 // DN_HEAD_DIM) == (c2 // DN_HEAD_DIM)
    state = [state_s[p] for p in range(len(pairs))]
    o_parts = [[] for _ in pairs]
    for c in chunks:
        r_last = (c + 1) * cs - 1
        v_new = {}
        for p, (ha, hb) in enumerate(pairs):
            wq = jnp.concatenate([w2[p, c].astype(BF16), side([qd[ha][rows_of(c)], qd[hb][rows_of(c)]])], axis=0)
            ws = _mm(wq, state[p])
            v_new[p] = u2[p, c] - ws[:cs]
            o_parts[p].append(ws[cs:] + _mm(attn[p, c], bdiag(v_new[p])))
        for p, (ha, hb) in enumerate(pairs):
            decay_last = jnp.concatenate(
                [jnp.broadcast_to(jnp.exp(gc[r_last:r_last + 1, LANE_ALPHA + hh:LANE_ALPHA + hh + 1]),
                                  (DN_HEAD_DIM, 1)) for hh in (ha, hb)], axis=0)
            ktv = _mm_tn(side([kt[ha][rows_of(c)], kt[hb][rows_of(c)]]), v_new[p])
            state[p] = state[p] * decay_last + jnp.where(on_blocks, ktv, 0.0)
    out_a = []
    for p, pair in enumerate(pairs):
        state_s[p] = state[p]
        o_pair = jnp.concatenate(o_parts[p], axis=0)
        for o_h, hh in zip(halves(o_pair), pair):
            z = proj[:, COL_Z + hh * DN_HEAD_DIM:COL_Z + (hh + 1) * DN_HEAD_DIM]
            out_a.append(_rmsnorm(o_h, dnw_ref[...]) * _silu(z))

    ri = lax.broadcasted_iota(jnp.int32, (SG_CHUNK, SG_CHUNK), 0)
    ci = lax.broadcasted_iota(jnp.int32, (SG_CHUNK, SG_CHUNK), 1)
    n_sg = tt // SG_CHUNK
    out_b = []
    for gg in range(SG_GROUPS):
        su = _gelu(proj[:, COL_SU + gg * SG_GROUP_DIM:COL_SU + (gg + 1) * SG_GROUP_DIM])
        sv = _gelu(proj[:, COL_SV + gg * SG_GROUP_DIM:COL_SV + (gg + 1) * SG_GROUP_DIM])
        sv = _rmsnorm(sv, sgn_ref[gg:gg + 1, :]).astype(BF16)
        w_s = jnp.where(ci <= ri, sgw_ref[gg], 0.0)
        sv_cat = jnp.concatenate([sv[n * SG_CHUNK:(n + 1) * SG_CHUNK, :] for n in range(n_sg)], axis=1)
        mixed_cat = _mm(w_s, sv_cat) + sgbt_ref[:, gg:gg + 1]
        mixed = jnp.concatenate([mixed_cat[:, n * SG_GROUP_DIM:(n + 1) * SG_GROUP_DIM] for n in range(n_sg)],
                                axis=0)
        out_b.append(su * mixed)

    o_ref[0] = x + _mm(jnp.concatenate(out_a + out_b, axis=-1), wout_ref[...])


def _ab_mixer(x, norm_w, w_in, conv_w, a_log, dt_bias, dn_norm, sg_norm, sg_w, sg_b, w_out, *, tt=512):
    bsz, t, _ = x.shape
    q, k, v, z, b, a, su, sv = jnp.split(w_in, (512, 1024, 1536, 2048, 2052, 2056, 2568), axis=-1)
    pad = jnp.zeros((D_MODEL, LANES - 2 * DN_HEADS), w_in.dtype)
    w_in_r = jnp.concatenate([q, k, v, z, su, sv, b, a, pad], axis=-1).astype(BF16)
    lane_pad = lambda p: jnp.zeros((1, LANES), F32).at[0, LANE_ALPHA:LANE_ALPHA + DN_HEADS].set(p)
    const2 = lambda bi, ti: (0, 0)
    const3 = lambda bi, ti: (0, 0, 0)
    once = dict(pipeline_mode=pl.Buffered(1))
    return pl.pallas_call(
        functools.partial(_ab_kernel, tt=tt),
        out_shape=jax.ShapeDtypeStruct(x.shape, F32),
        grid=(bsz, t // tt),
        in_specs=[
            pl.BlockSpec((1, tt, D_MODEL), lambda bi, ti: (bi, ti, 0)),
            pl.BlockSpec((1, D_MODEL), const2),
            pl.BlockSpec((D_MODEL, PROJ_COLS), const2, **once),
            pl.BlockSpec((DN_CONV, QKV_WIDTH), const2),
            pl.BlockSpec((1, LANES), const2),
            pl.BlockSpec((1, LANES), const2),
            pl.BlockSpec((1, DN_HEAD_DIM), const2),
            pl.BlockSpec((SG_GROUPS, SG_GROUP_DIM), const2),
            pl.BlockSpec((SG_GROUPS, SG_CHUNK, SG_CHUNK), const3),
            pl.BlockSpec((SG_CHUNK, SG_GROUPS), const2),
            pl.BlockSpec((D_MODEL, D_MODEL), const2, **once),
        ],
        out_specs=pl.BlockSpec((1, tt, D_MODEL), lambda bi, ti: (bi, ti, 0)),
        scratch_shapes=[
            pltpu.VMEM((tt + SUBLANES, QKV_WIDTH), F32),
            pltpu.VMEM((DN_HEADS // 2, 2 * DN_HEAD_DIM, 2 * DN_HEAD_DIM), F32),
        ],
        compiler_params=pltpu.CompilerParams(
            dimension_semantics=("arbitrary", "arbitrary"), vmem_limit_bytes=VMEM_LIMIT_BYTES),
        name="ab_mix",
    )(x, norm_w.reshape(1, D_MODEL), w_in_r, conv_w, lane_pad(a_log), lane_pad(dt_bias),
      dn_norm.reshape(1, DN_HEAD_DIM), sg_norm, sg_w, jnp.transpose(sg_b), w_out.astype(BF16))


def _pool_kernel(x_ref, nw_ref, pw_ref, ps_ref, o_ref, h_s, *, tt):
    ti = pl.program_id(1)

    @pl.when(ti == 0)
    def _():
        h_s[0:POOL_HALO, :] = jnp.zeros((POOL_HALO, D_MODEL), F32)

    x = x_ref[0]
    h = _rmsnorm(x, nw_ref[...])
    h_s[POOL_HALO:POOL_HALO + tt, :] = h
    pos = ti * tt + lax.broadcasted_iota(jnp.int32, (tt, 1), 0) + 1
    outs = []
    for gi, win in enumerate(POOL_WINDOWS):
        sl = slice(gi * POOL_GROUP_DIM, (gi + 1) * POOL_GROUP_DIM)
        hg = h[:, sl]
        acc = hg
        for s in range(1, win):
            acc = acc + h_s[POOL_HALO - s:POOL_HALO - s + tt, sl]
        count = jnp.minimum(pos, win).astype(F32)
        outs.append(_mm(acc / count - hg, pw_ref[gi]))
    h_s[0:POOL_HALO, :] = h_s[tt:tt + POOL_HALO, :]
    o_ref[0] = x + jnp.concatenate(outs, axis=-1) * ps_ref[...]


def _pool_mixer(x, norm_w, pool_w, pool_scale, *, tt=512):
    bsz, t, _ = x.shape
    const2 = lambda bi, ti: (0, 0)
    return pl.pallas_call(
        functools.partial(_pool_kernel, tt=tt),
        out_shape=jax.ShapeDtypeStruct(x.shape, F32),
        grid=(bsz, t // tt),
        in_specs=[
            pl.BlockSpec((1, tt, D_MODEL), lambda bi, ti: (bi, ti, 0)),
            pl.BlockSpec((1, D_MODEL), const2),
            pl.BlockSpec((len(POOL_WINDOWS), POOL_GROUP_DIM, POOL_GROUP_DIM), lambda bi, ti: (0, 0, 0)),
            pl.BlockSpec((1, D_MODEL), const2),
        ],
        out_specs=pl.BlockSpec((1, tt, D_MODEL), lambda bi, ti: (bi, ti, 0)),
        scratch_shapes=[pltpu.VMEM((tt + POOL_HALO, D_MODEL), F32)],
        compiler_params=pltpu.CompilerParams(
            dimension_semantics=("arbitrary", "arbitrary"), vmem_limit_bytes=VMEM_LIMIT_BYTES),
        name="pool_mix",
    )(x, norm_w.reshape(1, D_MODEL), pool_w.astype(BF16), pool_scale.reshape(1, D_MODEL))


def kernel(x, ffn_norm1, ffn1_w_in, ffn1_w_out, mix_norm, ffn_norm2, ffn2_w_in, ffn2_w_out, ab_w_in, dn_conv_w, dn_a_log, dn_dt_bias, dn_out_norm, sg_norm, sg_w, sg_b, ab_w_out, pool_w, pool_scale, final_norm):
    bsz, t, d = x.shape
    depth = ffn_norm1.shape[0]
    flat = lambda a: a.reshape(bsz * t, d)
    ffn1_w_in, ffn1_w_out = ffn1_w_in.astype(BF16), ffn1_w_out.astype(BF16)
    ffn2_w_in, ffn2_w_out = ffn2_w_in.astype(BF16), ffn2_w_out.astype(BF16)
    for l in range(depth):
        x = _ffn(flat(x), ffn_norm1, ffn1_w_in, ffn1_w_out, final_norm, l,
                 apply_final_norm=False).reshape(bsz, t, d)
        i = l // 2
        if l % 2 == 0:
            x = _ab_mixer(x, mix_norm[l], ab_w_in[i], dn_conv_w[i], dn_a_log[i], dn_dt_bias[i],
                          dn_out_norm[i], sg_norm[i], sg_w[i], sg_b[i], ab_w_out[i])
        else:
            x = _pool_mixer(x, mix_norm[l], pool_w[i], pool_scale[i])
        x = _ffn(flat(x), ffn_norm2, ffn2_w_in, ffn2_w_out, final_norm, l,
                 apply_final_norm=(l == depth - 1)).reshape(bsz, t, d)
    return x
```
